```python
import functools
import numpy as np
import jax
import jax.numpy as jnp
from jax import lax

D_MODEL = 2048
BATCH = 8
SEQ = 2048
DEPTH = 1
DEC_BATCH = 32
DEC_SEQ = 4
PAST_LEN = 8192
PAGE_SIZE = 128

D_CONV = 1024
CONV_W = 3
N_HEADS = 16
N_KV_HEADS = 4
HEAD_DIM = 128
GROUP = N_HEADS // N_KV_HEADS
ROT_DIM = HEAD_DIM // 4
N_IDX_HEADS = 16
IDX_DIM = 64
IDX_ROT_DIM = IDX_DIM // 4
TOPK_MAX = 256
ROPE_THETA = 500000.0
D_FF = 5632
Q_BLOCK = 128
LN_EPS = 1e-5
DEEPNORM_ALPHA = (2.0 * DEPTH) ** 0.25
DEEPNORM_BETA = (8.0 * DEPTH) ** -0.25
D_Q = N_HEADS * HEAD_DIM
D_KV = N_KV_HEADS * HEAD_DIM
D_IQ = N_IDX_HEADS * IDX_DIM
SPLIT_SIZES = (D_CONV, D_CONV, D_CONV, D_Q, D_KV, D_KV, D_IQ, IDX_DIM, N_IDX_HEADS, D_MODEL, D_MODEL)
D_IN = sum(SPLIT_SIZES)

kernel_name = 'dsa_shortconv_parallel_hybrid_step'


def _layer_norm(x, g, b):
    xf = x.astype(jnp.float32)
    mu = jnp.mean(xf, axis=-1, keepdims=True)
    var = jnp.mean(jnp.square(xf - mu), axis=-1, keepdims=True)
    y = (xf - mu) * lax.rsqrt(var + LN_EPS) * g.astype(jnp.float32) + b.astype(jnp.float32)
    return y.astype(x.dtype)


def _partial_rope(x, pos, rot_dim):
    half = rot_dim // 2
    inv_freq = ROPE_THETA ** (-jnp.arange(half, dtype=jnp.float32) / half)
    ang = pos.astype(jnp.float32)[:, None] * inv_freq[None, :]
    cos = jnp.cos(ang)[None, :, None, :]
    sin = jnp.sin(ang)[None, :, None, :]
    xr = x[..., :rot_dim].astype(jnp.float32)
    x1, x2 = xr[..., :half], xr[..., half:]
    rot = jnp.concatenate([x1 * cos - x2 * sin, x2 * cos + x1 * sin], axis=-1).astype(x.dtype)
    return jnp.concatenate([rot, x[..., rot_dim:]], axis=-1)


def _causal_dwconv(u, hist, w):
    T = u.shape[1]
    ext = jnp.concatenate([hist, u], axis=1)
    y = ext[:, 0:T] * w[0]
    for j in range(1, CONV_W):
        y = y + ext[:, j:j + T] * w[j]
    return y, ext[:, -(CONV_W - 1):]


def _index_scores(iq, ik, iw):
    s = jnp.einsum('bthd,bsd->bths', iq, ik, preferred_element_type=jnp.float32)
    return jnp.einsum('bths,bth->bts', jax.nn.relu(s * IDX_DIM ** -0.5), iw.astype(jnp.float32))


def _gather_rows(rows, idx):
    return jax.vmap(lambda r, i: r[i])(rows, idx)


def _sparse_attend(q, k_sel, v_sel, valid):
    B, T = q.shape[:2]
    qg = q.reshape(B, T, N_KV_HEADS, GROUP, HEAD_DIM)
    logits = jnp.einsum('btkgd,btskd->btkgs', qg, k_sel, preferred_element_type=jnp.float32) * HEAD_DIM ** -0.5
    logits = jnp.where(valid[:, :, None, None, :], logits, -jnp.inf)
    p = jax.nn.softmax(logits, axis=-1).astype(v_sel.dtype)
    o = jnp.einsum('btkgs,btskd->btkgd', p, v_sel)
    return o.reshape(B, T, D_Q)


def _prompt_attention(q, k, v, iq, ik, iw):
    B, T = q.shape[:2]
    topk = min(TOPK_MAX, T // 4)
    key_pos = jnp.arange(T, dtype=jnp.int32)

    def block(i):
        start = i * Q_BLOCK
        qb = lax.dynamic_slice_in_dim(q, start, Q_BLOCK, axis=1)
        iqb = lax.dynamic_slice_in_dim(iq, start, Q_BLOCK, axis=1)
        iwb = lax.dynamic_slice_in_dim(iw, start, Q_BLOCK, axis=1)
        qpos = start + jnp.arange(Q_BLOCK, dtype=jnp.int32)
        scores = _index_scores(iqb, ik, iwb)
        scores = jnp.where((key_pos[None, :] <= qpos[:, None])[None], scores, -jnp.inf)
        _, idx = lax.top_k(scores, topk)
        valid = idx <= qpos[None, :, None]
        return _sparse_attend(qb, _gather_rows(k, idx), _gather_rows(v, idx), valid)

    out = lax.map(block, jnp.arange(T // Q_BLOCK, dtype=jnp.int32))
    return jnp.swapaxes(out, 0, 1).reshape(B, T, D_Q)


def _sample_attention(q, k, v, iq, ik, iw, cache_k, cache_v, cache_idx_k, page_table, layer):
    B, T = q.shape[:2]
    n_pages = PAST_LEN // PAGE_SIZE
    L = PAST_LEN + T
    topk = min(TOPK_MAX, L // 4)
    ik_past = cache_idx_k[layer, page_table].reshape(B, n_pages * PAGE_SIZE, IDX_DIM)
    ik_all = jnp.concatenate([ik_past, ik], axis=1)
    qpos = PAST_LEN + jnp.arange(T, dtype=jnp.int32)
    scores = _index_scores(iq, ik_all, iw)
    scores = jnp.where((jnp.arange(L, dtype=jnp.int32)[None, :] <= qpos[:, None])[None], scores, -jnp.inf)
    _, idx = lax.top_k(scores, topk)
    valid = idx <= qpos[None, :, None]
    in_past = (idx < PAST_LEN)[..., None, None]
    pidx = jnp.minimum(idx, PAST_LEN - 1)
    phys = jax.vmap(lambda pt, i: pt[i])(page_table, pidx // PAGE_SIZE)
    off = pidx % PAGE_SIZE
    nidx = jnp.clip(idx - PAST_LEN, 0, T - 1)
    k_sel = jnp.where(in_past, cache_k[layer, phys, off], _gather_rows(k, nidx))
    v_sel = jnp.where(in_past, cache_v[layer, phys, off], _gather_rows(v, nidx))
    return _sparse_attend(q, k_sel, v_sel, valid)


def _layer(x, pos, attend, conv_a_hist, ffn_hist, w_in, idx_k_norm_g, idx_k_norm_b, conv_a_w, w_a_out,
           w_attn_out, w_mix_out, ln1_g, ln1_b, w_up, w_gate, conv_ffn_w, conv_ffn_b, w_down, ln2_g, ln2_b):
    B, T, _ = x.shape
    z = jnp.einsum('btd,de->bte', x, w_in)
    points = [int(p) for p in np.cumsum(SPLIT_SIZES)[:-1]]
    cb, cc, ch, q, k, v, iq, ik, iw, ga, gb = jnp.split(z, points, axis=-1)
    y_conv, new_conv_a = _causal_dwconv(cc * ch, conv_a_hist, conv_a_w)
    y_a = jnp.einsum('btc,cd->btd', cb * y_conv, w_a_out)
    q = _partial_rope(q.reshape(B, T, N_HEADS, HEAD_DIM), pos, ROT_DIM)
    k = _partial_rope(k.reshape(B, T, N_KV_HEADS, HEAD_DIM), pos, ROT_DIM)
    v = v.reshape(B, T, N_KV_HEADS, HEAD_DIM)
    iq = _partial_rope(iq.reshape(B, T, N_IDX_HEADS, IDX_DIM), pos, IDX_ROT_DIM)
    ik = _partial_rope(_layer_norm(ik, idx_k_norm_g, idx_k_norm_b)[:, :, None, :], pos, IDX_ROT_DIM)[:, :, 0, :]
    iw = iw * N_IDX_HEADS ** -0.5
    y_b = jnp.einsum('bte,ed->btd', attend(q, k, v, iq, ik, iw), w_attn_out)
    m = jax.nn.sigmoid(ga) * y_a + jax.nn.sigmoid(gb) * y_b
    h = _layer_norm(DEEPNORM_ALPHA * x + jnp.einsum('btd,de->bte', m, w_mix_out), ln1_g, ln1_b)
    u = jnp.einsum('btd,df->btf', h, w_up)
    g = jnp.einsum('btd,df->btf', h, w_gate)
    uc, new_conv_ffn = _causal_dwconv(u, ffn_hist, conv_ffn_w)
    f = jnp.einsum('btf,fd->btd', jax.nn.gelu(uc + conv_ffn_b) * g, w_down)
    out = _layer_norm(DEEPNORM_ALPHA * h + f, ln2_g, ln2_b)
    return out, k, v, ik, new_conv_a, new_conv_ffn


def setup_inputs(seed: int = 0) -> dict:
    key = jax.random.key(seed)
    ks = jax.random.split(key, 26)
    n_pages = PAST_LEN // PAGE_SIZE
    n_phys = (DEC_BATCH * n_pages * 5) // 4

    def nrm(k, shape, scale):
        return jax.random.normal(k, shape, jnp.float32) * scale

    x_prompt = nrm(ks[0], (BATCH, SEQ, D_MODEL), 1.0)
    x_sample = nrm(ks[1], (DEC_BATCH, DEC_SEQ, D_MODEL), 1.0)
    cache_k = nrm(ks[2], (DEPTH, n_phys, PAGE_SIZE, N_KV_HEADS, HEAD_DIM), 1.0)
    cache_v = nrm(ks[3], (DEPTH, n_phys, PAGE_SIZE, N_KV_HEADS, HEAD_DIM), 1.0)
    cache_idx_k = nrm(ks[4], (DEPTH, n_phys, PAGE_SIZE, IDX_DIM), 1.0)
    state_conv_a = nrm(ks[5], (DEPTH, DEC_BATCH, CONV_W - 1, D_CONV), 1.0)
    state_conv_ffn = nrm(ks[6], (DEPTH, DEC_BATCH, CONV_W - 1, D_FF), 1.0)
    page_table = jax.random.permutation(ks[7], n_phys)[:DEC_BATCH * n_pages].reshape(DEC_BATCH, n_pages).astype(jnp.int32)
    col_scale = jnp.concatenate([
        jnp.ones((3 * D_CONV + D_Q + D_KV,), jnp.float32),
        jnp.full((D_KV,), DEEPNORM_BETA, dtype=jnp.float32),
        jnp.ones((D_IQ + IDX_DIM + N_IDX_HEADS + 2 * D_MODEL,), jnp.float32)])
    w_in = nrm(ks[8], (DEPTH, D_MODEL, D_IN), D_MODEL ** -0.5) * col_scale
    idx_k_norm_g = 1.0 + nrm(ks[9], (DEPTH, IDX_DIM), 0.02)
    idx_k_norm_b = nrm(ks[10], (DEPTH, IDX_DIM), 0.02)
    conv_a_w = nrm(ks[11], (DEPTH, CONV_W, D_CONV), CONV_W ** -0.5)
    w_a_out = nrm(ks[12], (DEPTH, D_CONV, D_MODEL), DEEPNORM_BETA * D_CONV ** -0.5)
    w_attn_out = nrm(ks[13], (DEPTH, D_Q, D_MODEL), DEEPNORM_BETA * D_Q ** -0.5)
    w_mix_out = nrm(ks[14], (DEPTH, D_MODEL, D_MODEL), DEEPNORM_BETA * D_MODEL ** -0.5)
    ln1_g = 1.0 + nrm(ks[15], (DEPTH, D_MODEL), 0.02)
    ln1_b = nrm(ks[16], (DEPTH, D_MODEL), 0.02)
    w_up = nrm(ks[17], (DEPTH, D_MODEL, D_FF), D_MODEL ** -0.5)
    w_gate = nrm(ks[18], (DEPTH, D_MODEL, D_FF), D_MODEL ** -0.5)
    conv_ffn_w = nrm(ks[19], (DEPTH, CONV_W, D_FF), CONV_W ** -0.5)
    conv_ffn_b = nrm(ks[20], (DEPTH, D_FF), 0.02)
    w_down = nrm(ks[21], (DEPTH, D_FF, D_MODEL), DEEPNORM_BETA * D_FF ** -0.5)
    ln2_g = 1.0 + nrm(ks[22], (DEPTH, D_MODEL), 0.02)
    ln2_b = nrm(ks[23], (DEPTH, D_MODEL), 0.02)
    return {'x_prompt': x_prompt, 'x_sample': x_sample, 'cache_k': cache_k, 'cache_v': cache_v,
            'cache_idx_k': cache_idx_k, 'state_conv_a': state_conv_a, 'state_conv_ffn': state_conv_ffn,
            'page_table': page_table, 'w_in': w_in, 'idx_k_norm_g': idx_k_norm_g, 'idx_k_norm_b': idx_k_norm_b,
            'conv_a_w': conv_a_w, 'w_a_out': w_a_out, 'w_attn_out': w_attn_out, 'w_mix_out': w_mix_out,
            'ln1_g': ln1_g, 'ln1_b': ln1_b, 'w_up': w_up, 'w_gate': w_gate, 'conv_ffn_w': conv_ffn_w,
            'conv_ffn_b': conv_ffn_b, 'w_down': w_down, 'ln2_g': ln2_g, 'ln2_b': ln2_b}


def reference(x_prompt, x_sample, cache_k, cache_v, cache_idx_k, state_conv_a, state_conv_ffn, page_table,
              w_in, idx_k_norm_g, idx_k_norm_b, conv_a_w, w_a_out, w_attn_out, w_mix_out, ln1_g, ln1_b,
              w_up, w_gate, conv_ffn_w, conv_ffn_b, w_down, ln2_g, ln2_b):
    bp, tp = x_prompt.shape[:2]
    bs, ts = x_sample.shape[:2]
    pos_p = jnp.arange(tp, dtype=jnp.int32)
    pos_s = PAST_LEN + jnp.arange(ts, dtype=jnp.int32)
    hp, hs = x_prompt, x_sample
    kp, vp, ikp, cap, cfp = [], [], [], [], []
    kss, vss, iks, cas, cfs = [], [], [], [], []
    for l in range(DEPTH):
        wts = (w_in[l], idx_k_norm_g[l], idx_k_norm_b[l], conv_a_w[l], w_a_out[l], w_attn_out[l], w_mix_out[l],
               ln1_g[l], ln1_b[l], w_up[l], w_gate[l], conv_ffn_w[l], conv_ffn_b[l], w_down[l], ln2_g[l], ln2_b[l])
        zero_a = jnp.zeros((bp, CONV_W - 1, D_CONV), hp.dtype)
        zero_f = jnp.zeros((bp, CONV_W - 1, D_FF), hp.dtype)
        hp, k1, v1, ik1, ca1, cf1 = _layer(hp, pos_p, _prompt_attention, zero_a, zero_f, *wts)
        kp.append(k1); vp.append(v1); ikp.append(ik1); cap.append(ca1); cfp.append(cf1)
        attend_s = functools.partial(_sample_attention, cache_k=cache_k, cache_v=cache_v,
                                     cache_idx_k=cache_idx_k, page_table=page_table, layer=l)
        hs, k2, v2, ik2, ca2, cf2 = _layer(hs, pos_s, attend_s, state_conv_a[l], state_conv_ffn[l], *wts)
        kss.append(k2); vss.append(v2); iks.append(ik2); cas.append(ca2); cfs.append(cf2)
    return (hp, hs, jnp.stack(kp), jnp.stack(vp), jnp.stack(ikp), jnp.stack(cap), jnp.stack(cfp),
            jnp.stack(kss), jnp.stack(vss), jnp.stack(iks), jnp.stack(cas), jnp.stack(cfs))
```

```python
import functools
import math

import numpy as np
import jax
import jax.numpy as jnp
from jax import lax
from jax.experimental import pallas as pl
from jax.experimental.pallas import tpu as pltpu

F32 = jnp.float32
BF16 = jnp.bfloat16
I32 = jnp.int32

N_HEADS = 16
N_KV_HEADS = 4
GROUP = N_HEADS // N_KV_HEADS
HEAD_DIM = 128
ROT_DIM = HEAD_DIM // 4
N_IDX_HEADS = 16
IDX_DIM = 64
IDX_ROT_DIM = IDX_DIM // 4
TOPK_MAX = 256
ROPE_THETA = 500000.0
PAGE_SIZE = 128
CONV_W = 3
LN_EPS = 1e-5
D_Q = N_HEADS * HEAD_DIM
D_KV = N_KV_HEADS * HEAD_DIM
D_IQ = N_IDX_HEADS * IDX_DIM

LANES = 128
SUBLANES = 8
VMEM_LIMIT_BYTES = 56 * 1024 * 1024

INT_MIN = np.int32(-2 ** 31)
NEG_BIG = -1e30


def _call(body, *, grid, in_specs, out_specs, out_shape, scratch_shapes=(), num_scalar_prefetch=0):
    return pl.pallas_call(
        body,
        grid_spec=pltpu.PrefetchScalarGridSpec(
            num_scalar_prefetch=num_scalar_prefetch, grid=grid, in_specs=in_specs,
            out_specs=out_specs, scratch_shapes=scratch_shapes),
        out_shape=out_shape,
        compiler_params=pltpu.CompilerParams(
            dimension_semantics=("arbitrary",) * len(grid), vmem_limit_bytes=VMEM_LIMIT_BYTES),
    )


def _dot(a, b):
    return jnp.dot(a, b, preferred_element_type=F32)


def _dot_nt(a, b):
    return lax.dot_general(a, b, (((1,), (1,)), ((), ())), preferred_element_type=F32)


def _pick_tile(n, pref):
    t = min(n, pref)
    while n % t:
        t //= 2
    return t


def _rope(x, tab_ref, half):
    ctab = tab_ref[:, 0:LANES]
    s_up = tab_ref[:, LANES:2 * LANES]
    s_dn = tab_ref[:, 2 * LANES:3 * LANES]
    outs = []
    for c in range(x.shape[1] // LANES):
        xc = x[:, c * LANES:(c + 1) * LANES]
        up = pltpu.roll(xc, LANES - half, 1)
        dn = pltpu.roll(xc, half, 1)
        outs.append(xc * ctab + up * s_up + dn * s_dn)
    return outs[0] if len(outs) == 1 else jnp.concatenate(outs, axis=1)


def _causal_conv(p, w_ref, carry_ref, hist_refs, *, i, j, tiles_per_seq, seq_len, sample):
    tm = p.shape[0]
    rows = lax.broadcasted_iota(I32, p.shape, 0)
    r1 = pltpu.roll(p, 1, 0)
    r2 = pltpu.roll(p, 2, 0)
    if sample:
        t = lax.rem(rows, seq_len)
        h1_ref, h2_ref = hist_refs
        p1 = jnp.where(t >= 1, r1, 0.0) + h1_ref[...]
        p2 = jnp.where(t >= 2, r2, 0.0) + h2_ref[...]
    else:
        @pl.when(i % tiles_per_seq == 0)
        def _():
            carry_ref[j] = jnp.zeros(carry_ref.shape[1:], F32)

        prev = carry_ref[j]
        m1 = prev[SUBLANES - 1:SUBLANES, :]
        m2 = prev[SUBLANES - 2:SUBLANES - 1, :]
        p1 = jnp.where(rows == 0, m1, r1)
        p2 = jnp.where(rows == 0, m2, jnp.where(rows == 1, m1, r2))
        carry_ref[j] = p[tm - SUBLANES:, :]
    w = w_ref[...]
    return w[0:1, :] * p2 + w[1:2, :] * p1 + w[2:3, :] * p


def _layer_norm_rows(x, g, b):
    mu = jnp.mean(x, axis=-1, keepdims=True)
    d = x - mu
    var = jnp.mean(d * d, axis=-1, keepdims=True)
    return d * lax.rsqrt(var + LN_EPS) * g + b


def _gelu_tanh(x):
    c = math.sqrt(2.0 / math.pi)
    return 0.5 * x * (1.0 + jnp.tanh(c * (x + 0.044715 * (x * x * x))))


def _sigmoid(x):
    return 1.0 / (1.0 + jnp.exp(-x))


def _branch_a_kernel(*refs, tiles_per_seq, seq_len, sample):
    if sample:
        x_ref, wb_ref, wc_ref, wh_ref, cw_ref, h1_ref, h2_ref, a_ref, tail_ref, carry_ref = refs
        hist = (h1_ref, h2_ref)
    else:
        x_ref, wb_ref, wc_ref, wh_ref, cw_ref, a_ref, tail_ref, carry_ref = refs
        hist = None
    i, j = pl.program_id(0), pl.program_id(1)
    x = x_ref[...]
    cb = _dot(x, wb_ref[...])
    p = _dot(x, wc_ref[...]) * _dot(x, wh_ref[...])
    y = _causal_conv(p, cw_ref, carry_ref, hist, i=i, j=j, tiles_per_seq=tiles_per_seq,
                     seq_len=seq_len, sample=sample)
    a_ref[...] = (cb * y).astype(BF16)
    tail_ref[...] = p[p.shape[0] - tail_ref.shape[0]:, :]


def _branch_a(xb, w_a, conv_w, hist, *, seq_len, sample, tm):
    n, d = xb.shape
    dc = w_a.shape[1] // 3
    tn = _pick_tile(dc, 256)
    nj = dc // tn
    nm = n // tm
    tail_rows = tm if sample else SUBLANES
    in_specs = [
        pl.BlockSpec((tm, d), lambda i, j: (i, 0)),
        pl.BlockSpec((d, tn), lambda i, j: (0, j)),
        pl.BlockSpec((d, tn), lambda i, j: (0, nj + j)),
        pl.BlockSpec((d, tn), lambda i, j: (0, 2 * nj + j)),
        pl.BlockSpec((CONV_W, tn), lambda i, j: (0, j)),
    ]
    args = [xb, w_a, w_a, w_a, conv_w]
    if sample:
        in_specs += [pl.BlockSpec((tm, tn), lambda i, j: (i, j))] * 2
        args += list(hist)
    body = functools.partial(_branch_a_kernel, tiles_per_seq=max(seq_len // tm, 1), seq_len=seq_len,
                             sample=sample)
    return _call(
        body, grid=(nm, nj), in_specs=in_specs,
        out_specs=[pl.BlockSpec((tm, tn), lambda i, j: (i, j)),
                   pl.BlockSpec((tail_rows, tn), lambda i, j: (i, j))],
        out_shape=[jax.ShapeDtypeStruct((n, dc), BF16),
                   jax.ShapeDtypeStruct((nm * tail_rows, dc), F32)],
        scratch_shapes=[pltpu.VMEM((nj, SUBLANES, tn), F32)],
    )(*args)


def _q_kernel(x_ref, w_ref, tab_ref, q_ref):
    z = _dot(x_ref[...], w_ref[...])
    q_ref[...] = (_rope(z, tab_ref, ROT_DIM // 2) * (HEAD_DIM ** -0.5)).astype(BF16)


def _iq_kernel(x_ref, w_ref, tab_ref, q_ref):
    z = _dot(x_ref[...], w_ref[...])
    q_ref[...] = _rope(z, tab_ref, IDX_ROT_DIM // 2).astype(BF16)


def _rope_proj(kernel_fn, xb, w, tab, *, tm, tab_tiles):
    n, d = xb.shape
    dout = w.shape[1]
    tn = _pick_tile(dout, 512)
    return _call(
        kernel_fn, grid=(n // tm, dout // tn),
        in_specs=[pl.BlockSpec((tm, d), lambda i, j: (i, 0)),
                  pl.BlockSpec((d, tn), lambda i, j: (0, j)),
                  pl.BlockSpec((tm, 3 * LANES), lambda i, j: (i % tab_tiles, 0))],
        out_specs=pl.BlockSpec((tm, tn), lambda i, j: (i, j)),
        out_shape=jax.ShapeDtypeStruct((n, dout), BF16),
    )(xb, w, tab)


def _kv_kernel(x_ref, w_ref, tab_ref, k_ref, v_ref, kb_ref, vb_ref):
    j = pl.program_id(1)
    z = _dot(x_ref[...], w_ref[...])

    @pl.when(j == 0)
    def _():
        k = _rope(z, tab_ref, ROT_DIM // 2)
        k_ref[...] = k
        kb_ref[...] = k.astype(BF16)

    @pl.when(j == 1)
    def _():
        v_ref[...] = z
        vb_ref[...] = z.astype(BF16)


def _kv_proj(xb, w_kv, tab, *, tm, tab_tiles):
    n, d = xb.shape
    return _call(
        _kv_kernel, grid=(n // tm, 2),
        in_specs=[pl.BlockSpec((tm, d), lambda i, j: (i, 0)),
                  pl.BlockSpec((d, D_KV), lambda i, j: (0, j)),
                  pl.BlockSpec((tm, 3 * LANES), lambda i, j: (i % tab_tiles, 0))],
        out_specs=[pl.BlockSpec((tm, D_KV), lambda i, j: (i, 0))] * 4,
        out_shape=[jax.ShapeDtypeStruct((n, D_KV), F32), jax.ShapeDtypeStruct((n, D_KV), F32),
                   jax.ShapeDtypeStruct((n, D_KV), BF16), jax.ShapeDtypeStruct((n, D_KV), BF16)],
    )(xb, w_kv, tab)


def _ikw_kernel(x_ref, w_ref, tab_ref, g_ref, b_ref, ik_ref, iw_ref):
    z = _dot(x_ref[...], w_ref[...])
    lane = lax.broadcasted_iota(I32, z.shape, 1)
    is_k = lane < IDX_DIM
    mu = jnp.sum(jnp.where(is_k, z, 0.0), axis=-1, keepdims=True) * (1.0 / IDX_DIM)
    dlt = jnp.where(is_k, z - mu, 0.0)
    var = jnp.sum(dlt * dlt, axis=-1, keepdims=True) * (1.0 / IDX_DIM)
    y = dlt * lax.rsqrt(var + LN_EPS) * g_ref[...] + b_ref[...]
    y = _rope(y, tab_ref, IDX_ROT_DIM // 2)
    ik_ref[...] = y[:, 0:IDX_DIM]
    iw_ref[...] = z[:, IDX_DIM:IDX_DIM + N_IDX_HEADS] * (N_IDX_HEADS ** -0.5 * IDX_DIM ** -0.5)


def _ikw_proj(xb, w_ikw, tab, g128, b128, *, tm, tab_tiles):
    n, d = xb.shape
    return _call(
        _ikw_kernel, grid=(n // tm,),
        in_specs=[pl.BlockSpec((tm, d), lambda i: (i, 0)),
                  pl.BlockSpec((d, LANES), lambda i: (0, 0)),
                  pl.BlockSpec((tm, 3 * LANES), lambda i: (i % tab_tiles, 0)),
                  pl.BlockSpec((1, LANES), lambda i: (0, 0)),
                  pl.BlockSpec((1, LANES), lambda i: (0, 0))],
        out_specs=[pl.BlockSpec((tm, IDX_DIM), lambda i: (i, 0)),
                   pl.BlockSpec((tm, N_IDX_HEADS), lambda i: (i, 0))],
        out_shape=[jax.ShapeDtypeStruct((n, IDX_DIM), F32),
                   jax.ShapeDtypeStruct((n, N_IDX_HEADS), F32)],
    )(xb, w_ikw, tab, g128, b128)


def _gate_kernel(x_ref, w_ref, s_ref):
    s_ref[...] = _sigmoid(_dot(x_ref[...], w_ref[...])).astype(BF16)


def _gate_proj(xb, w_g, *, tm):
    n, d = xb.shape
    dout = w_g.shape[1]
    tn = _pick_tile(dout, 512)
    return _call(
        _gate_kernel, grid=(n // tm, dout // tn),
        in_specs=[pl.BlockSpec((tm, d), lambda i, j: (i, 0)),
                  pl.BlockSpec((d, tn), lambda i, j: (0, j))],
        out_specs=pl.BlockSpec((tm, tn), lambda i, j: (i, j)),
        out_shape=jax.ShapeDtypeStruct((n, dout), BF16),
    )(xb, w_g)


def _merge_kernel(a_ref, o_ref, wa_ref, wo_ref, sa_ref, sb_ref, m_ref):
    ya = _dot(a_ref[...], wa_ref[...])
    yb = _dot(o_ref[...], wo_ref[...])
    m_ref[...] = (sa_ref[...].astype(F32) * ya + sb_ref[...].astype(F32) * yb).astype(BF16)


def _merge(a, o, w_a_out, w_attn_out, sg, *, tm):
    n, dc = a.shape
    dq = o.shape[1]
    d = w_a_out.shape[1]
    tn = _pick_tile(d, 512)
    nj = d // tn
    return _call(
        _merge_kernel, grid=(n // tm, nj),
        in_specs=[pl.BlockSpec((tm, dc), lambda i, j: (i, 0)),
                  pl.BlockSpec((tm, dq), lambda i, j: (i, 0)),
                  pl.BlockSpec((dc, tn), lambda i, j: (0, j)),
                  pl.BlockSpec((dq, tn), lambda i, j: (0, j)),
                  pl.BlockSpec((tm, tn), lambda i, j: (i, j)),
                  pl.BlockSpec((tm, tn), lambda i, j: (i, nj + j))],
        out_specs=pl.BlockSpec((tm, tn), lambda i, j: (i, j)),
        out_shape=jax.ShapeDtypeStruct((n, d), BF16),
    )(a, o, w_a_out, w_attn_out, sg, sg)


def _mm_ln_kernel(a_ref, w_ref, res_ref, g_ref, b_ref, h_ref, hb_ref, r_ref, *, alpha, nj, tn):
    j = pl.program_id(1)
    r_ref[:, pl.ds(pl.multiple_of(j * tn, tn), tn)] = _dot(a_ref[...], w_ref[...])

    @pl.when(j == nj - 1)
    def _():
        y = _layer_norm_rows(alpha * res_ref[...] + r_ref[...], g_ref[...], b_ref[...])
        h_ref[...] = y
        hb_ref[...] = y.astype(BF16)


def _mm_ln(a, w, res, g, b, *, alpha, tm, tn_pref):
    n, k = a.shape
    d = w.shape[1]
    tn = _pick_tile(d, tn_pref)
    nj = d // tn
    body = functools.partial(_mm_ln_kernel, alpha=alpha, nj=nj, tn=tn)
    return _call(
        body, grid=(n // tm, nj),
        in_specs=[pl.BlockSpec((tm, k), lambda i, j: (i, 0)),
                  pl.BlockSpec((k, tn), lambda i, j: (0, j)),
                  pl.BlockSpec((tm, d), lambda i, j: (i, 0)),
                  pl.BlockSpec((1, d), lambda i, j: (0, 0)),
                  pl.BlockSpec((1, d), lambda i, j: (0, 0))],
        out_specs=[pl.BlockSpec((tm, d), lambda i, j: (i, 0))] * 2,
        out_shape=[jax.ShapeDtypeStruct((n, d), F32), jax.ShapeDtypeStruct((n, d), BF16)],
        scratch_shapes=[pltpu.VMEM((tm, d), F32)],
    )(a, w, res, g, b)


def _ffn1_kernel(*refs, tiles_per_seq, seq_len, sample):
    if sample:
        x_ref, wu_ref, wg_ref, cw_ref, cb_ref, h1_ref, h2_ref, act_ref, tail_ref, carry_ref = refs
        hist = (h1_ref, h2_ref)
    else:
        x_ref, wu_ref, wg_ref, cw_ref, cb_ref, act_ref, tail_ref, carry_ref = refs
        hist = None
    i, j = pl.program_id(0), pl.program_id(1)
    x = x_ref[...]
    u = _dot(x, wu_ref[...])
    g = _dot(x, wg_ref[...])
    uc = _causal_conv(u, cw_ref, carry_ref, hist, i=i, j=j, tiles_per_seq=tiles_per_seq,
                      seq_len=seq_len, sample=sample)
    act_ref[...] = (_gelu_tanh(uc + cb_ref[...]) * g).astype(BF16)
    tail_ref[...] = u[u.shape[0] - tail_ref.shape[0]:, :]


def _ffn1(hb, w_up, w_gate, conv_w, conv_b, hist, *, seq_len, sample, tm):
    n, d = hb.shape
    dff = w_up.shape[1]
    tn = _pick_tile(dff, 512)
    nj = dff // tn
    nm = n // tm
    tail_rows = tm if sample else SUBLANES
    in_specs = [
        pl.BlockSpec((tm, d), lambda i, j: (i, 0)),
        pl.BlockSpec((d, tn), lambda i, j: (0, j)),
        pl.BlockSpec((d, tn), lambda i, j: (0, j)),
        pl.BlockSpec((CONV_W, tn), lambda i, j: (0, j)),
        pl.BlockSpec((1, tn), lambda i, j: (0, j)),
    ]
    args = [hb, w_up, w_gate, conv_w, conv_b]
    if sample:
        in_specs += [pl.BlockSpec((tm, tn), lambda i, j: (i, j))] * 2
        args += list(hist)
    body = functools.partial(_ffn1_kernel, tiles_per_seq=max(seq_len // tm, 1), seq_len=seq_len,
                             sample=sample)
    return _call(
        body, grid=(nm, nj), in_specs=in_specs,
        out_specs=[pl.BlockSpec((tm, tn), lambda i, j: (i, j)),
                   pl.BlockSpec((tail_rows, tn), lambda i, j: (i, j))],
        out_shape=[jax.ShapeDtypeStruct((n, dff), BF16),
                   jax.ShapeDtypeStruct((nm * tail_rows, dff), F32)],
        scratch_shapes=[pltpu.VMEM((nj, SUBLANES, tn), F32)],
    )(*args)


def _score_keys(scores, admissible):
    bits = lax.bitcast_convert_type(scores, I32)
    key = bits ^ (lax.shift_right_arithmetic(bits, 31) & np.int32(0x7FFFFFFF))
    return jnp.where(admissible, key, INT_MIN)


def _select_topk(count_fn, rows, topk, index_bits):
    kf = jnp.float32(topk)

    def bit_body(it, t_u):
        bit = lax.shift_left(np.int32(1), (31 - it).astype(I32))
        cand = (t_u | bit) ^ INT_MIN
        cnt = count_fn(lambda key, col: key >= cand)
        return jnp.where(cnt >= kf, t_u | bit, t_u)

    t_u = lax.fori_loop(0, 32, bit_body, jnp.zeros((rows, 1), I32))
    thr = t_u ^ INT_MIN
    need = kf - count_fn(lambda key, col: key > thr)

    def cut_body(it, cut):
        bit = lax.shift_left(np.int32(1), (index_bits - 1 - it).astype(I32))
        cand = cut | bit
        cnt = count_fn(lambda key, col: (key == thr) & (col < cand))
        return jnp.where(cnt < need, cand, cut)

    cut = lax.fori_loop(0, index_bits, cut_body, jnp.zeros((rows, 1), I32))
    return thr, cut


def _prompt_attn_kernel(q_ref, iq_ref, iw_ref, ikt_ref, kt_ref, v_ref, o_ref,
                        key_ref, bias_ref, wb_ref, *, tq, kc, topk, index_bits):
    qi = pl.program_id(1)
    q0 = qi * tq
    n_chunks = (q0 + tq + kc - 1) // kc
    reps = kc // LANES

    for h in range(N_IDX_HEADS):
        wb_ref[h] = jnp.broadcast_to(iw_ref[:, h:h + 1], (tq, LANES))

    def chunk_off(c):
        return pl.multiple_of(c * kc, kc)

    def score_body(c, carry):
        off = chunk_off(c)
        ikc = ikt_ref[:, pl.ds(off, kc)]
        acc = jnp.zeros((tq, kc), F32)
        for h in range(N_IDX_HEADS):
            s = _dot(iq_ref[:, h * IDX_DIM:(h + 1) * IDX_DIM], ikc)
            wbh = wb_ref[h]
            acc = acc + jnp.maximum(s, 0.0) * jnp.concatenate([wbh] * reps, axis=1)
        col = off + lax.broadcasted_iota(I32, (tq, kc), 1)
        row = q0 + lax.broadcasted_iota(I32, (tq, kc), 0)
        key_ref[:, pl.ds(off, kc)] = _score_keys(acc, col <= row)
        return carry

    lax.fori_loop(0, n_chunks, score_body, 0)

    def count_fn(pred):
        def body(c, part):
            off = chunk_off(c)
            key = key_ref[:, pl.ds(off, kc)]
            col = off + lax.broadcasted_iota(I32, (tq, kc), 1)
            hit = jnp.where(pred(key, col), 1.0, 0.0)
            for r in range(reps):
                part = part + hit[:, r * LANES:(r + 1) * LANES]
            return part
        part = lax.fori_loop(0, n_chunks, body, jnp.zeros((tq, LANES), F32))
        return jnp.sum(part, axis=1, keepdims=True)

    thr, cut = _select_topk(count_fn, tq, topk, index_bits)

    def bias_body(c, carry):
        off = chunk_off(c)
        key = key_ref[:, pl.ds(off, kc)]
        col = off + lax.broadcasted_iota(I32, (tq, kc), 1)
        sel = ((key > thr) | ((key == thr) & (col <= cut))) & (key != INT_MIN)
        bias_ref[:, pl.ds(off, kc)] = jnp.where(sel, 0.0, NEG_BIG)
        return carry

    lax.fori_loop(0, n_chunks, bias_body, 0)

    for g in range(N_KV_HEADS):
        qg = jnp.concatenate(
            [q_ref[:, (g * GROUP + j) * HEAD_DIM:(g * GROUP + j + 1) * HEAD_DIM] for j in range(GROUP)],
            axis=0)

        def attn_body(c, carry, g=g, qg=qg):
            m, l, acc = carry
            off = chunk_off(c)
            s = _dot(qg, kt_ref[g, :, pl.ds(off, kc)])
            s = s + jnp.concatenate([bias_ref[:, pl.ds(off, kc)]] * GROUP, axis=0)
            m_new = jnp.maximum(m, jnp.max(s, axis=1, keepdims=True))
            alpha = jnp.exp(m - m_new)
            p = jnp.exp(s - m_new)
            l = alpha * l + jnp.sum(p, axis=1, keepdims=True)
            pv = _dot(p.astype(BF16), v_ref[pl.ds(off, kc), g * HEAD_DIM:(g + 1) * HEAD_DIM])
            return m_new, l, alpha * acc + pv

        init = (jnp.full((GROUP * tq, 1), NEG_BIG, F32), jnp.zeros((GROUP * tq, 1), F32),
                jnp.zeros((GROUP * tq, HEAD_DIM), F32))
        _, l, acc = lax.fori_loop(0, n_chunks, attn_body, init)
        out = acc / l
        for j in range(GROUP):
            h = g * GROUP + j
            o_ref[:, h * HEAD_DIM:(h + 1) * HEAD_DIM] = out[j * tq:(j + 1) * tq, :].astype(BF16)


def _prompt_attention(q, iq, iw, ikt, kt, vb, *, batch, seq):
    n = q.shape[0]
    tq = _pick_tile(seq, 128)
    kc = _pick_tile(seq, 512)
    nq = seq // tq
    topk = min(TOPK_MAX, seq // 4)
    index_bits = max(1, (seq - 1).bit_length())
    body = functools.partial(_prompt_attn_kernel, tq=tq, kc=kc, topk=topk, index_bits=index_bits)
    return _call(
        body, grid=(batch, nq),
        in_specs=[pl.BlockSpec((tq, D_Q), lambda b, i: (b * nq + i, 0)),
                  pl.BlockSpec((tq, D_IQ), lambda b, i: (b * nq + i, 0)),
                  pl.BlockSpec((tq, N_IDX_HEADS), lambda b, i: (b * nq + i, 0)),
                  pl.BlockSpec((None, IDX_DIM, seq), lambda b, i: (b, 0, 0)),
                  pl.BlockSpec((None, N_KV_HEADS, HEAD_DIM, seq), lambda b, i: (b, 0, 0, 0)),
                  pl.BlockSpec((seq, D_KV), lambda b, i: (b, 0))],
        out_specs=pl.BlockSpec((tq, D_Q), lambda b, i: (b * nq + i, 0)),
        out_shape=jax.ShapeDtypeStruct((n, D_Q), BF16),
        scratch_shapes=[pltpu.VMEM((tq, seq), I32), pltpu.VMEM((tq, seq), F32),
                        pltpu.VMEM((N_IDX_HEADS, tq, LANES), F32)],
    )(q, iq, iw, ikt, kt, vb)


def _row_group_select(vals, rows_total):
    row = lax.broadcasted_iota(I32, vals[0].shape, 0)
    grp = lax.shift_right_logical(row & (N_HEADS - 1), int(math.log2(GROUP)))
    out = vals[N_KV_HEADS - 1]
    for g in range(N_KV_HEADS - 2, -1, -1):
        out = jnp.where(grp == g, vals[g], out)
    return out


def _sample_score_kernel(pt_ref, iq_ref, wb_ref, *rest, n_pages_step, dec_seq):
    page_refs = rest[:n_pages_step]
    sc_ref = rest[n_pages_step]
    iq = iq_ref[...]
    wb = wb_ref[...]
    for u in range(n_pages_step):
        page = page_refs[u][...].astype(BF16)
        r = jnp.maximum(_dot_nt(iq, page), 0.0) * wb
        sc_ref[:, u * PAGE_SIZE:(u + 1) * PAGE_SIZE] = jnp.sum(
            r.reshape(dec_seq, N_IDX_HEADS, PAGE_SIZE), axis=1)


def _sample_scores(page_table, iq_s, wb_s, cache_ik2, *, n_pages_step):
    bs, rows, _ = iq_s.shape
    dec_seq = rows // N_IDX_HEADS
    n_pages = page_table.shape[1]
    steps = n_pages // n_pages_step
    page_specs = [
        pl.BlockSpec((PAGE_SIZE, IDX_DIM),
                     functools.partial(lambda b, s, pt, u: (pt[b, s * n_pages_step + u], 0), u=u))
        for u in range(n_pages_step)]
    body = functools.partial(_sample_score_kernel, n_pages_step=n_pages_step, dec_seq=dec_seq)
    return _call(
        body, grid=(bs, steps), num_scalar_prefetch=1,
        in_specs=[pl.BlockSpec((None, rows, IDX_DIM), lambda b, s, pt: (b, 0, 0)),
                  pl.BlockSpec((None, rows, LANES), lambda b, s, pt: (b, 0, 0))] + page_specs,
        out_specs=pl.BlockSpec((None, dec_seq, n_pages_step * PAGE_SIZE), lambda b, s, pt: (b, 0, s)),
        out_shape=jax.ShapeDtypeStruct((bs, dec_seq, n_pages * PAGE_SIZE), F32),
    )(page_table, iq_s, wb_s, *([cache_ik2] * n_pages_step))


def _sample_attn_kernel(pt_ref, sc_ref, q_ref, iq_ref, wb_ref, iktn_ref, ktn_ref, vn_ref, *rest,
                        n_pages_step, steps, dec_seq, past_len, topk, index_bits):
    k_refs = rest[:n_pages_step]
    v_refs = rest[n_pages_step:2 * n_pages_step]
    o_ref, logit_ref, bias_ref, key_ref, acc_ref, l_ref = rest[2 * n_pages_step:]
    s = pl.program_id(1)
    rows = dec_seq * N_HEADS
    total = past_len + LANES
    q = q_ref[...]

    @pl.when(s == 0)
    def _():
        r = jnp.maximum(_dot(iq_ref[...], iktn_ref[...]), 0.0) * wb_ref[...]
        sc_new = jnp.sum(r.reshape(dec_seq, N_IDX_HEADS, LANES), axis=1)
        lane = lax.broadcasted_iota(I32, (dec_seq, LANES), 1)
        trow = lax.broadcasted_iota(I32, (dec_seq, LANES), 0)
        key_ref[:, 0:past_len] = _score_keys(sc_ref[...], True)
        key_ref[:, past_len:total] = _score_keys(sc_new, lane <= trow)

        def count_fn(pred):
            col = lax.broadcasted_iota(I32, (dec_seq, total), 1)
            return jnp.sum(jnp.where(pred(key_ref[...], col), 1.0, 0.0), axis=1, keepdims=True)

        thr, cut = _select_topk(count_fn, dec_seq, topk, index_bits)
        key = key_ref[...]
        col = lax.broadcasted_iota(I32, (dec_seq, total), 1)
        sel = ((key > thr) | ((key == thr) & (col <= cut))) & (key != INT_MIN)
        bias = jnp.where(sel, 0.0, NEG_BIG)
        for t in range(dec_seq):
            bias_ref[t * N_HEADS:(t + 1) * N_HEADS, :] = jnp.broadcast_to(bias[t:t + 1, :], (N_HEADS, total))
        logit_ref[:, past_len:total] = _row_group_select(
            [_dot(q, ktn_ref[g]) for g in range(N_KV_HEADS)], rows)

    @pl.when(s < steps)
    def _():
        for u in range(n_pages_step):
            vals = []
            for g in range(N_KV_HEADS):
                kg = k_refs[u][pl.ds(g, PAGE_SIZE, stride=N_KV_HEADS), :].astype(BF16)
                vals.append(_dot_nt(q, kg))
            off = pl.multiple_of((s * n_pages_step + u) * PAGE_SIZE, PAGE_SIZE)
            logit_ref[:, pl.ds(off, PAGE_SIZE)] = _row_group_select(vals, rows)

    @pl.when(s == steps)
    def _():
        x = logit_ref[...] + bias_ref[...]
        m = jnp.max(x, axis=1, keepdims=True)
        p = jnp.exp(x - m)
        l_ref[...] = jnp.sum(p, axis=1, keepdims=True)
        logit_ref[...] = p
        pn = p[:, past_len:total].astype(BF16)
        acc_ref[...] = _row_group_select([_dot(pn, vn_ref[g]) for g in range(N_KV_HEADS)], rows)

    @pl.when(s >= steps)
    def _():
        acc = acc_ref[...]
        for u in range(n_pages_step):
            off = pl.multiple_of(((s - steps) * n_pages_step + u) * PAGE_SIZE, PAGE_SIZE)
            p = logit_ref[:, pl.ds(off, PAGE_SIZE)].astype(BF16)
            vals = []
            for g in range(N_KV_HEADS):
                vg = v_refs[u][pl.ds(g, PAGE_SIZE, stride=N_KV_HEADS), :].astype(BF16)
                vals.append(_dot(p, vg))
            acc = acc + _row_group_select(vals, rows)
        acc_ref[...] = acc

    @pl.when(s == 2 * steps - 1)
    def _():
        o_ref[...] = (acc_ref[...] / l_ref[...]).astype(BF16)


def _sample_attention(page_table, scores, q_s, iq_s, wb_s, iktn, ktn, vn, cache_k2, cache_v2, *,
                      n_pages_step, past_len):
    bs, rows, _ = q_s.shape
    dec_seq = rows // N_HEADS
    n_pages = page_table.shape[1]
    steps = n_pages // n_pages_step
    total = past_len + LANES
    topk = min(TOPK_MAX, (past_len + dec_seq) // 4)
    index_bits = max(1, (total - 1).bit_length())
    page_rows = PAGE_SIZE * N_KV_HEADS

    def k_map(b, s, pt, u):
        return (pt[b, jnp.minimum(s, steps - 1) * n_pages_step + u], 0)

    def v_map(b, s, pt, u):
        return (pt[b, jnp.maximum(s - steps, 0) * n_pages_step + u], 0)

    k_specs = [pl.BlockSpec((page_rows, HEAD_DIM), functools.partial(k_map, u=u)) for u in range(n_pages_step)]
    v_specs = [pl.BlockSpec((page_rows, HEAD_DIM), functools.partial(v_map, u=u)) for u in range(n_pages_step)]
    body = functools.partial(_sample_attn_kernel, n_pages_step=n_pages_step, steps=steps, dec_seq=dec_seq,
                             past_len=past_len, topk=topk, index_bits=index_bits)
    return _call(
        body, grid=(bs, 2 * steps), num_scalar_prefetch=1,
        in_specs=[pl.BlockSpec((None, dec_seq, past_len), lambda b, s, pt: (b, 0, 0)),
                  pl.BlockSpec((None, rows, HEAD_DIM), lambda b, s, pt: (b, 0, 0)),
                  pl.BlockSpec((None, rows, IDX_DIM), lambda b, s, pt: (b, 0, 0)),
                  pl.BlockSpec((None, rows, LANES), lambda b, s, pt: (b, 0, 0)),
                  pl.BlockSpec((None, IDX_DIM, LANES), lambda b, s, pt: (b, 0, 0)),
                  pl.BlockSpec((None, N_KV_HEADS, HEAD_DIM, LANES), lambda b, s, pt: (b, 0, 0, 0)),
                  pl.BlockSpec((None, N_KV_HEADS, LANES, HEAD_DIM), lambda b, s, pt: (b, 0, 0, 0))]
                 + k_specs + v_specs,
        out_specs=pl.BlockSpec((None, rows, HEAD_DIM), lambda b, s, pt: (b, 0, 0)),
        out_shape=jax.ShapeDtypeStruct((bs, rows, HEAD_DIM), BF16),
        scratch_shapes=[pltpu.VMEM((rows, total), F32), pltpu.VMEM((rows, total), F32),
                        pltpu.VMEM((dec_seq, total), I32), pltpu.VMEM((rows, HEAD_DIM), F32),
                        pltpu.VMEM((rows, 1), F32)],
    )(page_table, scores, q_s, iq_s, wb_s, iktn, ktn, vn, *([cache_k2] * n_pages_step),
      *([cache_v2] * n_pages_step))


def _rope_tables(pos, rot_dim, period):
    half = rot_dim // 2
    inv_freq = ROPE_THETA ** (-jnp.arange(half, dtype=F32) / half)
    ang = pos.astype(F32)[:, None] * inv_freq[None, :]
    cos, sin = jnp.cos(ang), jnp.sin(ang)
    t = pos.shape[0]
    rest = period - rot_dim
    c = jnp.concatenate([cos, cos, jnp.ones((t, rest), F32)], axis=1)
    s_up = jnp.concatenate([-sin, jnp.zeros((t, half + rest), F32)], axis=1)
    s_dn = jnp.concatenate([jnp.zeros((t, half), F32), sin, jnp.zeros((t, rest), F32)], axis=1)
    reps = LANES // period
    return jnp.concatenate([jnp.tile(c, (1, reps)), jnp.tile(s_up, (1, reps)), jnp.tile(s_dn, (1, reps))], axis=1)


def _history_rows(state, seq_len):
    b, _, c = state.shape
    z = jnp.zeros((b, seq_len - 2, c), F32) if seq_len > 2 else None
    h1 = [state[:, 1:2], jnp.zeros((b, 1, c), F32)] + ([z] if z is not None else [])
    h2 = [state[:, 0:1], state[:, 1:2]] + ([z] if z is not None else [])
    return (jnp.concatenate(h1, axis=1).reshape(b * seq_len, c),
            jnp.concatenate(h2, axis=1).reshape(b * seq_len, c))


def _layer_tokens(x2, weights, tabs, *, seq_len, sample, tm, hist_a, hist_f, attend):
    (w_a, w_q, w_kv, w_iq, w_ikw, w_g, g_ik, b_ik, conv_a_w, w_a_out, w_attn_out, w_mix_out,
     ln1_g, ln1_b, w_up, w_gate, conv_ffn_w, conv_ffn_b, w_down, ln2_g, ln2_b, alpha) = weights
    tab_head, tab_idx = tabs
    tab_tiles = tab_head.shape[0] // tm
    xb = x2.astype(BF16)
    a_act, tail_a = _branch_a(xb, w_a, conv_a_w, hist_a, seq_len=seq_len, sample=sample, tm=tm)
    q = _rope_proj(_q_kernel, xb, w_q, tab_head, tm=tm, tab_tiles=tab_tiles)
    kf, vf, kb, vb = _kv_proj(xb, w_kv, tab_head, tm=tm, tab_tiles=tab_tiles)
    iq = _rope_proj(_iq_kernel, xb, w_iq, tab_idx, tm=tm, tab_tiles=tab_tiles)
    ik, iw = _ikw_proj(xb, w_ikw, tab_idx, g_ik, b_ik, tm=tm, tab_tiles=tab_tiles)
    sg = _gate_proj(xb, w_g, tm=tm)
    o = attend(q, kb, vb, iq, ik, iw)
    m = _merge(a_act, o, w_a_out, w_attn_out, sg, tm=tm)
    h, hb = _mm_ln(m, w_mix_out, x2, ln1_g, ln1_b, alpha=alpha, tm=tm, tn_pref=512)
    act, tail_f = _ffn1(hb, w_up, w_gate, conv_ffn_w, conv_ffn_b, hist_f, seq_len=seq_len, sample=sample, tm=tm)
    out, _ = _mm_ln(act, w_down, h, ln2_g, ln2_b, alpha=alpha, tm=tm, tn_pref=256)
    return out, kf, vf, ik, tail_a, tail_f


def kernel(x_prompt, x_sample, cache_k, cache_v, cache_idx_k, state_conv_a, state_conv_ffn, page_table, w_in,
           idx_k_norm_g, idx_k_norm_b, conv_a_w, w_a_out, w_attn_out, w_mix_out, ln1_g, ln1_b, w_up, w_gate,
           conv_ffn_w, conv_ffn_b, w_down, ln2_g, ln2_b):
    depth = w_in.shape[0]
    assert depth == 1, "single-layer step"
    bp, tp, d = x_prompt.shape
    bs, ts, _ = x_sample.shape
    assert ts >= CONV_W - 1
    dc = conv_a_w.shape[-1]
    dff = w_up.shape[-1]
    n_pages = page_table.shape[1]
    past_len = n_pages * PAGE_SIZE
    alpha = (2.0 * depth) ** 0.25
    l = 0

    o_q = 3 * dc
    o_iq = o_q + D_Q + 2 * D_KV
    o_ik = o_iq + D_IQ
    o_g = o_ik + IDX_DIM + N_IDX_HEADS
    wi = w_in[l]
    w_a = wi[:, :o_q].astype(BF16)
    w_q = wi[:, o_q:o_q + D_Q].astype(BF16)
    w_kv = wi[:, o_q + D_Q:o_iq].astype(BF16)
    w_iq = wi[:, o_iq:o_ik].astype(BF16)
    w_ikw = jnp.pad(wi[:, o_ik:o_g], ((0, 0), (0, LANES - IDX_DIM - N_IDX_HEADS))).astype(BF16)
    w_g = wi[:, o_g:].astype(BF16)
    g_ik = jnp.pad(idx_k_norm_g[l], (0, LANES - IDX_DIM))[None, :]
    b_ik = jnp.pad(idx_k_norm_b[l], (0, LANES - IDX_DIM))[None, :]
    weights = (w_a, w_q, w_kv, w_iq, w_ikw, w_g, g_ik, b_ik, conv_a_w[l], w_a_out[l].astype(BF16),
               w_attn_out[l].astype(BF16), w_mix_out[l].astype(BF16), ln1_g[l][None, :], ln1_b[l][None, :],
               w_up[l].astype(BF16), w_gate[l].astype(BF16), conv_ffn_w[l], conv_ffn_b[l][None, :],
               w_down[l].astype(BF16), ln2_g[l][None, :], ln2_b[l][None, :], alpha)

    pos_p = jnp.arange(tp, dtype=I32)
    tabs_p = (_rope_tables(pos_p, ROT_DIM, HEAD_DIM), _rope_tables(pos_p, IDX_ROT_DIM, IDX_DIM))
    tm_p = _pick_tile(tp, 512)

    def attend_prompt(q, kb, vb, iq, ik, iw):
        ikt = jnp.swapaxes(ik.reshape(bp, tp, IDX_DIM), 1, 2).astype(BF16)
        kt = jnp.transpose(kb.reshape(bp, tp, N_KV_HEADS, HEAD_DIM), (0, 2, 3, 1))
        return _prompt_attention(q, iq, iw, ikt, kt, vb, batch=bp, seq=tp)

    yp, kp, vp, ikp, tail_ap, tail_fp = _layer_tokens(
        x_prompt.reshape(bp * tp, d), weights, tabs_p, seq_len=tp, sample=False, tm=tm_p,
        hist_a=None, hist_f=None, attend=attend_prompt)
    tiles_per_seq = tp // tm_p

    def prompt_state(tail):
        c = tail.shape[1]
        return tail.reshape(bp, tiles_per_seq, SUBLANES, c)[:, -1, SUBLANES - (CONV_W - 1):, :][None]

    pos_s = past_len + jnp.arange(ts, dtype=I32)
    pos_rows = jnp.tile(pos_s, bs)
    tabs_s = (_rope_tables(pos_rows, ROT_DIM, HEAD_DIM), _rope_tables(pos_rows, IDX_ROT_DIM, IDX_DIM))
    n_s = bs * ts
    n_phys = cache_k.shape[1]
    cache_ik2 = cache_idx_k[l].reshape(n_phys * PAGE_SIZE, IDX_DIM)
    cache_k2 = cache_k[l].reshape(n_phys * PAGE_SIZE * N_KV_HEADS, HEAD_DIM)
    cache_v2 = cache_v[l].reshape(n_phys * PAGE_SIZE * N_KV_HEADS, HEAD_DIM)
    n_pages_step = _pick_tile(n_pages, 8)

    def attend_sample(q, kb, vb, iq, ik, iw):
        rows = ts * N_HEADS
        q_s = q.reshape(bs, rows, HEAD_DIM)
        iq_s = iq.reshape(bs, rows, IDX_DIM)
        wb_s = jnp.broadcast_to(iw.reshape(bs, rows, 1), (bs, rows, LANES))
        scores = _sample_scores(page_table, iq_s, wb_s, cache_ik2, n_pages_step=n_pages_step)
        pad = LANES - ts
        iktn = jnp.pad(jnp.swapaxes(ik.reshape(bs, ts, IDX_DIM), 1, 2), ((0, 0), (0, 0), (0, pad))).astype(BF16)
        k4 = kb.reshape(bs, ts, N_KV_HEADS, HEAD_DIM)
        ktn = jnp.pad(jnp.transpose(k4, (0, 2, 3, 1)), ((0, 0), (0, 0), (0, 0), (0, pad)))
        v4 = vb.reshape(bs, ts, N_KV_HEADS, HEAD_DIM)
        vn = jnp.pad(jnp.transpose(v4, (0, 2, 1, 3)), ((0, 0), (0, 0), (0, pad), (0, 0)))
        o = _sample_attention(page_table, scores, q_s, iq_s, wb_s, iktn, ktn, vn, cache_k2, cache_v2,
                              n_pages_step=n_pages_step, past_len=past_len)
        return o.reshape(n_s, D_Q)

    ys, ks, vs, iks, tail_as, tail_fs = _layer_tokens(
        x_sample.reshape(n_s, d), weights, tabs_s, seq_len=ts, sample=True, tm=n_s,
        hist_a=_history_rows(state_conv_a[l], ts), hist_f=_history_rows(state_conv_ffn[l], ts),
        attend=attend_sample)

    def sample_state(tail):
        return tail.reshape(bs, ts, tail.shape[1])[:, ts - (CONV_W - 1):, :][None]

    return (yp.reshape(bp, tp, d), ys.reshape(bs, ts, d),
            kp.reshape(1, bp, tp, N_KV_HEADS, HEAD_DIM), vp.reshape(1, bp, tp, N_KV_HEADS, HEAD_DIM),
            ikp.reshape(1, bp, tp, IDX_DIM), prompt_state(tail_ap), prompt_state(tail_fp),
            ks.reshape(1, bs, ts, N_KV_HEADS, HEAD_DIM), vs.reshape(1, bs, ts, N_KV_HEADS, HEAD_DIM),
            iks.reshape(1, bs, ts, IDX_DIM), sample_state(tail_as), sample_state(tail_fs))
```
